```python
import functools
import jax, jax.numpy as jnp
from jax import lax
import numpy as np

D_MODEL = 1024
BATCH = 8
SEQ = 4096
DEPTH = 2
DEC_BATCH = 128
DEC_SEQ = 8
PAST_LEN = 16384
PAGE_SIZE = 128

HEAD_DIM = 64
CONV_WIDTH = 3
A_WIDTH = D_MODEL // 4
B_HEADS = (3 * D_MODEL // 8) // HEAD_DIM
C_Q_HEADS = (3 * D_MODEL // 8) // HEAD_DIM
C_KV_HEADS = C_Q_HEADS // 3
B_WIDTH = B_HEADS * HEAD_DIM
C_WIDTH = C_Q_HEADS * HEAD_DIM
C_KV_WIDTH = C_KV_HEADS * HEAD_DIM
MIX_WIDTH = A_WIDTH + B_WIDTH + C_WIDTH
IN_WIDTH = 3 * A_WIDTH + 3 * B_WIDTH + C_WIDTH + 2 * C_KV_WIDTH
DILATIONS = ((128, 1), (512, 4), (2048, 16))
B_WINDOW_MAX = 2048
C_WINDOW = 128
BLOCK = 128
D_FF = -(-8 * D_MODEL // (3 * 256)) * 256
EPS = 1e-6
ATTN_SCALE = HEAD_DIM ** -0.5

kernel_name = "hybrid_conv_dilated_swa_decoder_step"


def rmsnorm(x, g):
    x32 = x.astype(jnp.float32)
    y = x32 * lax.rsqrt(jnp.mean(x32 * x32, axis=-1, keepdims=True) + EPS)
    return (y * g.astype(jnp.float32)).astype(x.dtype)


def adaln(c, w_mod, b_mod):
    mod = jax.nn.silu(c) @ w_mod + b_mod
    return jnp.split(mod[:, None, :], 6, axis=-1)


def swiglu(h, w_gate_up, w_down):
    gate, up = jnp.split(h @ w_gate_up, 2, axis=-1)
    return (jax.nn.silu(gate) * up) @ w_down


def split_points():
    sizes = [A_WIDTH] * 3 + [B_WIDTH] * 3 + [C_WIDTH, C_KV_WIDTH, C_KV_WIDTH]
    return [int(v) for v in np.cumsum(sizes)[:-1]]


def project(h, w_in):
    n, s, _ = h.shape
    ga, gc, xa, qb, kb, vb, qc, kc, vc = jnp.split(h @ w_in, split_points(), axis=-1)
    heads = lambda t: t.reshape(n, s, -1, HEAD_DIM)
    return ga, gc, xa, heads(qb), heads(kb), heads(vb), heads(qc), heads(kc), heads(vc)


def short_conv(u_ext, w, n_out):
    return sum(w[i] * u_ext[:, i:i + n_out] for i in range(CONV_WIDTH))


def banded_attention(q, k, v, max_dist, sinks=None):
    n, L, hq, hd = q.shape
    hkv = k.shape[2]
    rep = hq // hkv
    nb = L // BLOCK
    qb = q.reshape(n, nb, BLOCK, hkv, rep, hd)

    def with_prev(t):
        t = t.reshape(n, nb, BLOCK, hkv, hd)
        prev = jnp.pad(t[:, :-1], ((0, 0), (1, 0), (0, 0), (0, 0), (0, 0)))
        return jnp.concatenate([prev, t], axis=2)

    kk, vv = with_prev(k), with_prev(v)
    s = jnp.einsum('nbqgrd,nbkgd->nbgrqk', qb, kk, preferred_element_type=jnp.float32) * ATTN_SCALE
    qi = jnp.arange(BLOCK)[:, None]
    kj = jnp.arange(2 * BLOCK)[None, :]
    dist = BLOCK + qi - kj
    kpos = jnp.arange(nb)[:, None, None] * BLOCK + kj[None] - BLOCK
    valid = (dist >= 0) & (dist <= max_dist) & (kpos >= 0)
    s = jnp.where(valid[None, :, None, None], s, -jnp.inf)
    m = s.max(-1)
    if sinks is not None:
        sk = sinks.astype(jnp.float32).reshape(hkv, rep)[None, None, :, :, None]
        m = jnp.maximum(m, sk)
    p = jnp.exp(s - m[..., None])
    l = p.sum(-1)
    if sinks is not None:
        l = l + jnp.exp(sk - m)
    o = jnp.einsum('nbgrqk,nbkgd->nbqgrd', p.astype(v.dtype), vv, preferred_element_type=jnp.float32)
    to_rows = lambda t: jnp.moveaxis(t, -1, 2).reshape(n, L, hq)
    m, l = to_rows(m), to_rows(l)
    o = o.reshape(n, L, hq, hd) / l[..., None]
    return o.astype(q.dtype), m, l


def combine_by_denominator(outs, ms, ls):
    m_max = functools.reduce(jnp.maximum, ms)
    ws = [l * jnp.exp(m - m_max) for m, l in zip(ms, ls)]
    total = sum(ws)
    y = sum(w[..., None] * o.astype(jnp.float32) for w, o in zip(ws, outs)) / total[..., None]
    return y.astype(outs[0].dtype)


def dilated_prompt(q, k, v):
    b, s, h, hd = q.shape
    outs, ms, ls = [], [], []
    for window, d in DILATIONS:
        L = s // d
        Lp = -(-L // BLOCK) * BLOCK

        def to_phase(t):
            t = t.reshape(b, L, d, h, hd).transpose(0, 2, 1, 3, 4).reshape(b * d, L, h, hd)
            return jnp.pad(t, ((0, 0), (0, Lp - L), (0, 0), (0, 0)))

        def from_phase(t):
            t = t[:, :L].reshape((b, d, L) + t.shape[2:])
            return jnp.swapaxes(t, 1, 2).reshape((b, s) + t.shape[3:])

        o, m, l = banded_attention(to_phase(q), to_phase(k), to_phase(v), window // d)
        outs.append(from_phase(o)); ms.append(from_phase(m)); ls.append(from_phase(l))
    return combine_by_denominator(outs, ms, ls)


def dilated_sample(q, k_cat, v_cat, n_past):
    ds = q.shape[1]
    q_idx = n_past + jnp.arange(ds)
    outs, ms, ls = [], [], []
    for window, d in DILATIONS:
        idx = q_idx[:, None] - d * jnp.arange(window // d + 1)[None, :]
        valid = idx >= 0
        idx = jnp.maximum(idx, 0)
        kg = k_cat[:, idx]
        vg = v_cat[:, idx]
        s = jnp.einsum('bqhd,bqkhd->bqhk', q, kg, preferred_element_type=jnp.float32) * ATTN_SCALE
        s = jnp.where(valid[None, :, None, :], s, -jnp.inf)
        m = s.max(-1)
        p = jnp.exp(s - m[..., None])
        l = p.sum(-1)
        o = jnp.einsum('bqhk,bqkhd->bqhd', p.astype(v_cat.dtype), vg, preferred_element_type=jnp.float32)
        outs.append((o / l[..., None]).astype(q.dtype)); ms.append(m); ls.append(l)
    return combine_by_denominator(outs, ms, ls)


def window_sample(q, k_cat, v_cat, n_past, sinks):
    db, ds, hq, hd = q.shape
    hkv = k_cat.shape[2]
    rep = hq // hkv
    dist = (n_past + jnp.arange(ds))[:, None] - jnp.arange(k_cat.shape[1])[None, :]
    valid = (dist >= 0) & (dist < C_WINDOW)
    s = jnp.einsum('bqgrd,bkgd->bgrqk', q.reshape(db, ds, hkv, rep, hd), k_cat,
                   preferred_element_type=jnp.float32) * ATTN_SCALE
    s = jnp.where(valid, s, -jnp.inf)
    sk = sinks.astype(jnp.float32).reshape(hkv, rep)[None, :, :, None]
    m = jnp.maximum(s.max(-1), sk)
    p = jnp.exp(s - m[..., None])
    l = p.sum(-1) + jnp.exp(sk - m)
    o = jnp.einsum('bgrqk,bkgd->bqgrd', p.astype(v_cat.dtype), v_cat, preferred_element_type=jnp.float32)
    o = o / jnp.moveaxis(l, -1, 1)[..., None]
    return o.reshape(db, ds, hq, hd).astype(q.dtype)


def mix_prompt(h, w_in, conv_w, sinks):
    n, s, _ = h.shape
    ga, gc, xa, qb, kb, vb, qc, kc, vc = project(h, w_in)
    u = gc * xa
    u_ext = jnp.pad(u, ((0, 0), (CONV_WIDTH - 1, 0), (0, 0)))
    ya = ga * short_conv(u_ext, conv_w, s)
    yb = dilated_prompt(qb, kb, vb)
    yc, _, _ = banded_attention(qc, kc, vc, C_WINDOW - 1, sinks)
    mix = jnp.concatenate([ya, yb.reshape(n, s, B_WIDTH), yc.reshape(n, s, C_WIDTH)], axis=-1)
    wb, wc = min(B_WINDOW_MAX, s), min(C_WINDOW, s)
    states = (u[:, s - (CONV_WIDTH - 1):], kb[:, s - wb:], vb[:, s - wb:], kc[:, s - wc:], vc[:, s - wc:])
    return mix, states


def mix_sample(h, conv_buf, bk_buf, bv_buf, ck_buf, cv_buf, w_in, conv_w, sinks):
    n, s, _ = h.shape
    ga, gc, xa, qb, kb, vb, qc, kc, vc = project(h, w_in)
    u_ext = jnp.concatenate([conv_buf, gc * xa], axis=1)
    ya = ga * short_conv(u_ext, conv_w, s)
    kb_cat = jnp.concatenate([bk_buf, kb], axis=1)
    vb_cat = jnp.concatenate([bv_buf, vb], axis=1)
    yb = dilated_sample(qb, kb_cat, vb_cat, bk_buf.shape[1])
    kc_cat = jnp.concatenate([ck_buf, kc], axis=1)
    vc_cat = jnp.concatenate([cv_buf, vc], axis=1)
    yc = window_sample(qc, kc_cat, vc_cat, ck_buf.shape[1], sinks)
    mix = jnp.concatenate([ya, yb.reshape(n, s, B_WIDTH), yc.reshape(n, s, C_WIDTH)], axis=-1)
    tb, tc, tu = kb_cat.shape[1], kc_cat.shape[1], u_ext.shape[1]
    wb, wc = min(B_WINDOW_MAX, tb), min(C_WINDOW, tc)
    states = (u_ext[:, tu - (CONV_WIDTH - 1):], kb_cat[:, tb - wb:], vb_cat[:, tb - wb:],
              kc_cat[:, tc - wc:], vc_cat[:, tc - wc:])
    return mix, states


def apply_layer(x, mod, mixer, g_mix, g_ffn, w_out, w_gate_up, w_down):
    shift1, scale1, gate1, shift2, scale2, gate2 = mod
    mix, states = mixer(rmsnorm(x, g_mix) * (1 + scale1) + shift1)
    x = x + gate1 * (mix @ w_out)
    x = x + gate2 * swiglu(rmsnorm(x, g_ffn) * (1 + scale2) + shift2, w_gate_up, w_down)
    return x, states


def setup_inputs(seed: int = 0) -> dict:
    key = jax.random.key(seed)
    ks = iter(jax.random.split(key, 32))
    nrm = lambda shape, scale=1.0: jax.random.normal(next(ks), shape, jnp.float32) * scale
    wb = min(B_WINDOW_MAX, PAST_LEN)
    wc = min(C_WINDOW, PAST_LEN)
    return {
        "x_prompt": nrm((BATCH, SEQ, D_MODEL)),
        "x_sample": nrm((DEC_BATCH, DEC_SEQ, D_MODEL)),
        "state_conv": nrm((DEPTH, DEC_BATCH, CONV_WIDTH - 1, A_WIDTH)),
        "cache_b_k": nrm((DEPTH, DEC_BATCH, wb, B_HEADS, HEAD_DIM)),
        "cache_b_v": nrm((DEPTH, DEC_BATCH, wb, B_HEADS, HEAD_DIM)),
        "cache_c_k": nrm((DEPTH, DEC_BATCH, wc, C_KV_HEADS, HEAD_DIM)),
        "cache_c_v": nrm((DEPTH, DEC_BATCH, wc, C_KV_HEADS, HEAD_DIM)),
        "c_prompt": nrm((BATCH, D_MODEL)),
        "c_sample": nrm((DEC_BATCH, D_MODEL)),
        "w_mod": nrm((DEPTH, D_MODEL, 6 * D_MODEL), 0.5 * D_MODEL ** -0.5),
        "b_mod": nrm((DEPTH, 6 * D_MODEL), 0.02),
        "norm_mix": 1.0 + nrm((DEPTH, D_MODEL), 0.02),
        "norm_ffn": 1.0 + nrm((DEPTH, D_MODEL), 0.02),
        "w_in": nrm((DEPTH, D_MODEL, IN_WIDTH), D_MODEL ** -0.5),
        "conv_w": nrm((DEPTH, CONV_WIDTH, A_WIDTH), CONV_WIDTH ** -0.5),
        "sinks": nrm((DEPTH, C_Q_HEADS)),
        "w_out": nrm((DEPTH, MIX_WIDTH, D_MODEL), MIX_WIDTH ** -0.5),
        "w_gate_up": nrm((DEPTH, D_MODEL, 2 * D_FF), D_MODEL ** -0.5),
        "w_down": nrm((DEPTH, D_FF, D_MODEL), D_FF ** -0.5),
        "norm_final": 1.0 + nrm((D_MODEL,), 0.02),
    }


def reference(x_prompt, x_sample, state_conv, cache_b_k, cache_b_v, cache_c_k, cache_c_v,
              c_prompt, c_sample, w_mod, b_mod, norm_mix, norm_ffn, w_in, conv_w, sinks,
              w_out, w_gate_up, w_down, norm_final):
    y_p, y_s = x_prompt, x_sample
    p_states, s_states = [], []
    for layer in range(DEPTH):
        mod_p = adaln(c_prompt, w_mod[layer], b_mod[layer])
        mod_s = adaln(c_sample, w_mod[layer], b_mod[layer])
        shared = (norm_mix[layer], norm_ffn[layer], w_out[layer], w_gate_up[layer], w_down[layer])
        y_p, st = apply_layer(
            y_p, mod_p,
            lambda h: mix_prompt(h, w_in[layer], conv_w[layer], sinks[layer]),
            *shared)
        p_states.append(st)
        y_s, st = apply_layer(
            y_s, mod_s,
            lambda h: mix_sample(h, state_conv[layer], cache_b_k[layer], cache_b_v[layer],
                                 cache_c_k[layer], cache_c_v[layer], w_in[layer], conv_w[layer], sinks[layer]),
            *shared)
        s_states.append(st)
    y_prompt = rmsnorm(y_p, norm_final)
    y_sample = rmsnorm(y_s, norm_final)
    conv_p, bk_p, bv_p, ck_p, cv_p = [jnp.stack(t) for t in zip(*p_states)]
    conv_s, bk_s, bv_s, ck_s, cv_s = [jnp.stack(t) for t in zip(*s_states)]
    return (y_prompt, y_sample, conv_p, conv_s, bk_p, bk_s, bv_p, bv_s, ck_p, ck_s, cv_p, cv_s)
```

```python
import functools

import numpy as np
import jax
import jax.numpy as jnp
from jax import lax
from jax.experimental import pallas as pl
from jax.experimental.pallas import tpu as pltpu

D_MODEL = 1024
DEPTH = 2
HEAD_DIM = 64
A_WIDTH = 256
B_WIDTH = 384
C_WIDTH = 384
C_KV_WIDTH = 128
IN_WIDTH = 2560
ATT_WIDTH = 1792
D_FF = 2816
FF_CHUNK = 256
BLOCK = 128
WB = 2048
WC = 128
EPS = 1e-6
ATTN_SCALE = HEAD_DIM ** -0.5
DILATIONS = ((128, 1), (512, 4), (2048, 16))
NEG_INF = float("-inf")
F32 = jnp.float32
BF16 = jnp.bfloat16
NT_DIMS = (((1,), (1,)), ((), ()))
VMEM_LIMIT = 56 * 1024 * 1024


def _params(n_axes, vmem=VMEM_LIMIT):
    return pltpu.CompilerParams(dimension_semantics=("arbitrary",) * n_axes, vmem_limit_bytes=vmem)


def _rms_mod(x, g, scale, shift):
    ms = jnp.mean(x * x, axis=-1, keepdims=True)
    return (x * lax.rsqrt(ms + EPS) * g) * (1.0 + scale) + shift


def _mod_kernel(c_ref, w_ref, b_ref, o_ref):
    c = c_ref[...]
    a = (c * jax.nn.sigmoid(c)).astype(BF16)
    o_ref[0] = jnp.dot(a, w_ref[0].astype(BF16), preferred_element_type=F32) + b_ref[0]


def _modulation(c_all, w_mod, b_mod):
    n = c_all.shape[0]
    return pl.pallas_call(
        _mod_kernel,
        grid=(DEPTH, 6),
        in_specs=[pl.BlockSpec((n, D_MODEL), lambda l, k: (0, 0)),
                  pl.BlockSpec((1, D_MODEL, D_MODEL), lambda l, k: (l, 0, k)),
                  pl.BlockSpec((1, 1, D_MODEL), lambda l, k: (l, 0, k))],
        out_specs=pl.BlockSpec((1, n, D_MODEL), lambda l, k: (l, 0, k)),
        out_shape=jax.ShapeDtypeStruct((DEPTH, n, 6 * D_MODEL), F32),
        compiler_params=_params(2),
        name="adaln_mod",
    )(c_all, w_mod, b_mod.reshape(DEPTH, 1, 6 * D_MODEL))


def _inproj_kernel(x_ref, shift_ref, scale_ref, g_ref, w_ref, cw_ref, st_ref,
                   ya_ref, att_ref, ust_ref, carry_ref):
    nb, tt, _ = x_ref.shape
    m = nb * tt

    @pl.when(pl.program_id(1) == 0)
    def _():
        carry_ref[...] = st_ref[...]

    h = _rms_mod(x_ref[...], g_ref[...], scale_ref[...], shift_ref[...])
    hb = h.reshape(m, D_MODEL).astype(BF16)

    def proj(lo, hi):
        return jnp.dot(hb, w_ref[:, lo:hi], preferred_element_type=F32)

    pa = proj(0, 3 * A_WIDTH)
    ga = pa[:, 0:A_WIDTH].reshape(nb, tt, A_WIDTH)
    u = (pa[:, A_WIDTH:2 * A_WIDTH] * pa[:, 2 * A_WIDTH:3 * A_WIDTH]).reshape(nb, tt, A_WIDTH)
    c0 = carry_ref[:, 0:1, :]
    c1 = carry_ref[:, 1:2, :]
    row = lax.broadcasted_iota(jnp.int32, u.shape, 1)
    u1 = jnp.where(row == 0, c1, pltpu.roll(u, 1, 1))
    u2 = jnp.where(row == 0, c0, jnp.where(row == 1, c1, pltpu.roll(u, 2, 1)))
    cw = cw_ref[...]
    conv = cw[0:1, :] * u2 + cw[1:2, :] * u1 + cw[2:3, :] * u
    ya_ref[...] = (ga * conv).astype(BF16)
    last = u[:, tt - 2:tt, :]
    carry_ref[...] = last
    ust_ref[...] = last

    att_ref[:, :, 0:384] = (proj(768, 1152) * ATTN_SCALE).reshape(nb, tt, 384)
    att_ref[:, :, 384:1152] = proj(1152, 1920).reshape(nb, tt, 768)
    att_ref[:, :, 1152:1536] = (proj(1920, 2304) * ATTN_SCALE).reshape(nb, tt, 384)
    att_ref[:, :, 1536:1792] = proj(2304, 2560).reshape(nb, tt, 256)


def _inproj(x, mod3, g, w_in, conv_w, state, *, nb, tt):
    n, s, _ = x.shape
    grid = (n // nb, s // tt)
    tok = lambda width: pl.BlockSpec((nb, tt, width), lambda b, t: (b, t, 0))
    modspec = lambda k: pl.BlockSpec((nb, 1, D_MODEL), lambda b, t: (b, 0, k))
    per_seq = pl.BlockSpec((nb, 2, A_WIDTH), lambda b, t: (b, 0, 0))
    return pl.pallas_call(
        _inproj_kernel,
        grid=grid,
        in_specs=[tok(D_MODEL), modspec(0), modspec(1),
                  pl.BlockSpec((1, D_MODEL), lambda b, t: (0, 0)),
                  pl.BlockSpec((D_MODEL, IN_WIDTH), lambda b, t: (0, 0)),
                  pl.BlockSpec((3, A_WIDTH), lambda b, t: (0, 0)),
                  per_seq],
        out_specs=[tok(A_WIDTH), tok(ATT_WIDTH), per_seq],
        out_shape=[jax.ShapeDtypeStruct((n, s, A_WIDTH), BF16),
                   jax.ShapeDtypeStruct((n, s, ATT_WIDTH), F32),
                   jax.ShapeDtypeStruct((n, 2, A_WIDTH), F32)],
        scratch_shapes=[pltpu.VMEM((nb, 2, A_WIDTH), F32)],
        compiler_params=_params(2),
        name="inproj_conv",
    )(x, mod3, mod3, g, w_in, conv_w, state)


def _prompt_attn_kernel(sinks_ref, qb_ref, kb_ref, vb_ref, qc_ref, kc_ref, vc_ref,
                        yb_ref, yc_ref, acc_ref, m0_ref, m1_ref, l0_ref, l1_ref):
    j = pl.program_id(1)
    n_blocks = qb_ref.shape[1] // BLOCK
    lane = lax.broadcasted_iota(jnp.int32, (BLOCK, BLOCK), 1)
    head0 = lane < HEAD_DIM
    qi = lax.broadcasted_iota(jnp.int32, (BLOCK, 2 * BLOCK), 0)
    kj = lax.broadcasted_iota(jnp.int32, (BLOCK, 2 * BLOCK), 1)
    in_cur = kj >= BLOCK
    causal_cur = (kj - BLOCK) <= qi
    band_b = (in_cur & causal_cur) | ((kj < BLOCK) & (kj >= qi))
    band_c = (in_cur & causal_cur) | ((kj < BLOCK) & (kj > qi))

    def split_heads(q):
        return (jnp.where(head0, q, 0.0).astype(BF16), jnp.where(head0, 0.0, q).astype(BF16))

    def rep2(a):
        return jnp.concatenate([a, a], axis=1)

    def dilation_pass(d, first):
        per_phase = n_blocks // d
        shift = per_phase.bit_length() - 1

        def rows_of(start):
            if d == 1:
                return pl.ds(pl.multiple_of(start, BLOCK), BLOCK)
            return pl.ds(start, BLOCK, stride=d)

        def body(idx, carry):
            phase = idx >> shift
            mb = idx & (per_phase - 1)
            rows = rows_of(phase + d * BLOCK * mb)
            prev = rows_of(phase + d * BLOCK * jnp.maximum(mb - 1, 0))
            q0, q1 = split_heads(qb_ref[0, rows, :])
            kk = jnp.concatenate([kb_ref[0, prev, :], kb_ref[0, rows, :]], axis=0).astype(BF16)
            vv = jnp.concatenate([vb_ref[0, prev, :], vb_ref[0, rows, :]], axis=0).astype(BF16)
            mask = band_b & (kj >= jnp.where(mb > 0, 0, BLOCK))
            acc_old = None if first else acc_ref[rows, :]
            outs = []
            for qh, m_ref, l_ref in ((q0, m0_ref, l0_ref), (q1, m1_ref, l1_ref)):
                s = lax.dot_general(qh, kk, NT_DIMS, preferred_element_type=F32)
                s = jnp.where(mask, s, NEG_INF)
                m_cur = jnp.max(s, axis=1, keepdims=True)
                if first:
                    m_new = jnp.broadcast_to(m_cur, (BLOCK, BLOCK))
                else:
                    m_old = m_ref[rows, :]
                    m_new = jnp.maximum(m_old, m_cur)
                p = jnp.exp(s - rep2(m_new))
                l_cur = jnp.sum(p, axis=1, keepdims=True)
                o = jnp.dot(p.astype(BF16), vv, preferred_element_type=F32)
                if first:
                    l_new = jnp.broadcast_to(l_cur, (BLOCK, BLOCK))
                else:
                    alpha = jnp.exp(m_old - m_new)
                    l_new = alpha * l_ref[rows, :] + l_cur
                    o = alpha * acc_old + o
                m_ref[rows, :] = m_new
                l_ref[rows, :] = l_new
                outs.append(o)
            acc_ref[rows, :] = jnp.where(head0, outs[0], outs[1])
            return carry

        lax.fori_loop(0, n_blocks, body, 0)

    for i, (_, d) in enumerate(DILATIONS):
        dilation_pass(d, i == 0)

    def finish(mb, carry):
        rows = pl.ds(pl.multiple_of(mb * BLOCK, BLOCK), BLOCK)
        denom = jnp.where(head0, l0_ref[rows, :], l1_ref[rows, :])
        yb_ref[0, rows, :] = (acc_ref[rows, :] / denom).astype(BF16)
        return carry

    lax.fori_loop(0, n_blocks, finish, 0)

    swap0 = j == 2
    swap1 = j == 0
    sink0 = sinks_ref[2 * j]
    sink1 = sinks_ref[2 * j + 1]

    def window_body(mb, carry):
        rows = pl.ds(pl.multiple_of(mb * BLOCK, BLOCK), BLOCK)
        prev = pl.ds(pl.multiple_of(jnp.maximum(mb - 1, 0) * BLOCK, BLOCK), BLOCK)
        q0, q1 = split_heads(qc_ref[0, rows, :])
        kk = jnp.concatenate([kc_ref[0, prev, :], kc_ref[0, rows, :]], axis=0)
        vv = jnp.concatenate([vc_ref[0, prev, :], vc_ref[0, rows, :]], axis=0)
        kk_sw = pltpu.roll(kk, HEAD_DIM, 1)
        vv_sw = pltpu.roll(vv, HEAD_DIM, 1)
        mask = band_c & (kj >= jnp.where(mb > 0, 0, BLOCK))
        outs = []
        for qh, swap, sink in ((q0, swap0, sink0), (q1, swap1, sink1)):
            kh = jnp.where(swap, kk_sw, kk).astype(BF16)
            vh = jnp.where(swap, vv_sw, vv).astype(BF16)
            s = lax.dot_general(qh, kh, NT_DIMS, preferred_element_type=F32)
            s = jnp.where(mask, s, NEG_INF)
            m = jnp.maximum(jnp.max(s, axis=1, keepdims=True), sink)
            p = jnp.exp(s - m)
            l = jnp.sum(p, axis=1, keepdims=True) + jnp.exp(sink - m)
            o = jnp.dot(p.astype(BF16), vh, preferred_element_type=F32)
            outs.append(o / l)
        yc_ref[0, rows, :] = jnp.where(head0, outs[0], outs[1]).astype(BF16)
        return carry

    lax.fori_loop(0, n_blocks, window_body, 0)


def _prompt_attn(att, sinks):
    n, s, _ = att.shape
    lanes = lambda f: pl.BlockSpec((1, s, BLOCK), f)
    out = pl.BlockSpec((1, s, BLOCK), lambda b, j: (b, 0, j))
    return pl.pallas_call(
        _prompt_attn_kernel,
        grid=(n, 3),
        in_specs=[pl.BlockSpec(memory_space=pltpu.SMEM),
                  lanes(lambda b, j: (b, 0, j)), lanes(lambda b, j: (b, 0, 3 + j)),
                  lanes(lambda b, j: (b, 0, 6 + j)), lanes(lambda b, j: (b, 0, 9 + j)),
                  lanes(lambda b, j: (b, 0, 12)), lanes(lambda b, j: (b, 0, 13))],
        out_specs=[out, out],
        out_shape=[jax.ShapeDtypeStruct((n, s, B_WIDTH), BF16),
                   jax.ShapeDtypeStruct((n, s, C_WIDTH), BF16)],
        scratch_shapes=[pltpu.VMEM((s, BLOCK), F32)] * 5,
        compiler_params=_params(2),
        name="prompt_attn",
    )(sinks, att, att, att, att, att, att)


def _sample_masks(ds):
    i = np.arange(ds)[:, None]
    def with_new(old, new_of_delta):
        col = np.arange(BLOCK)[None, :] - (BLOCK - ds)
        delta = i - col
        new = np.where((col >= 0) & (delta >= 0), new_of_delta(np.maximum(delta, 0)), 0.0)
        return np.concatenate([old, new], axis=1).astype(np.float32)
    def mult(delta):
        return sum(((delta <= w) & (delta % d == 0)).astype(np.float32) for w, d in DILATIONS)
    mb = with_new(mult(WB + i - np.arange(WB)[None, :]), mult)
    dist = WC + i - np.arange(WC)[None, :]
    mc = with_new((dist < WC).astype(np.float32), lambda delta: np.ones_like(delta, np.float32))
    return jnp.asarray(mb), jnp.asarray(mc)


def _sample_attn_kernel(layer, sinks_ref, att_ref, kt_ref, vt_ref, ckt_ref, cvt_ref, mb_ref, mc_ref,
                        *rest):
    yb_ref, yc_ref, okt_ref, ovt_ref, ockt_ref, ocvt_ref = rest[-6:]
    del layer
    ds = att_ref.shape[1]
    att = att_ref[0]
    lane = lax.broadcasted_iota(jnp.int32, (ds, BLOCK), 1)
    lane_sq = lax.broadcasted_iota(jnp.int32, (BLOCK, BLOCK), 1)
    is_new = lane_sq >= BLOCK - ds
    pad_rows = jnp.zeros((BLOCK - ds, BLOCK), F32)

    def new_cols(x):
        return jnp.concatenate([pad_rows, x], axis=0).T

    def shifted(old, new_t):
        w = old.shape[1]
        rolled = pltpu.roll(old, w - ds, 1)
        tail = jnp.where(is_new, new_t, rolled[:, w - BLOCK:])
        return rolled, tail

    def attend(qm, kt, knt, vt, vnt, weight, sink):
        s = jnp.concatenate([jnp.dot(qm, kt, preferred_element_type=F32),
                             jnp.dot(qm, knt, preferred_element_type=F32)], axis=1)
        s = jnp.where(weight > 0.0, s, NEG_INF)
        m = jnp.max(s, axis=1, keepdims=True)
        if sink is not None:
            m = jnp.maximum(m, sink)
        p = weight * jnp.exp(s - m)
        l = jnp.sum(p, axis=1, keepdims=True)
        if sink is not None:
            l = l + jnp.exp(sink - m)
        w = kt.shape[1]
        pb = p.astype(BF16)
        o = (lax.dot_general(pb[:, :w], vt, NT_DIMS, preferred_element_type=F32)
             + lax.dot_general(pb[:, w:], vnt, NT_DIMS, preferred_element_type=F32))
        return o / l

    w_b = mb_ref[...]
    for j in range(3):
        rows = slice(BLOCK * j, BLOCK * (j + 1))
        q2 = att[:, BLOCK * j:BLOCK * (j + 1)]
        knt = new_cols(att[:, 384 + BLOCK * j:384 + BLOCK * (j + 1)])
        vnt = new_cols(att[:, 768 + BLOCK * j:768 + BLOCK * (j + 1)])
        kt = kt_ref[0, 0, rows, :]
        vt = vt_ref[0, 0, rows, :]
        bf = tuple(a.astype(BF16) for a in (kt, knt, vt, vnt))
        outs = []
        for e in range(2):
            in_head = (lane >= HEAD_DIM * e) & (lane < HEAD_DIM * (e + 1))
            qm = jnp.where(in_head, q2, 0.0).astype(BF16)
            outs.append(attend(qm, *bf, w_b, None))
        yb_ref[0, :, rows] = jnp.where(lane < HEAD_DIM, outs[0], outs[1]).astype(BF16)
        for old, new_t, o_ref in ((kt, knt, okt_ref), (vt, vnt, ovt_ref)):
            rolled, tail = shifted(old, new_t)
            o_ref[0, 0, rows, :] = rolled
            o_ref[0, 0, rows, WB - BLOCK:WB] = tail

    w_c = mc_ref[...]
    ckt = ckt_ref[0, 0]
    cvt = cvt_ref[0, 0]
    cknt = new_cols(att[:, 1536:1664])
    cvnt = new_cols(att[:, 1664:1792])
    bf = tuple(a.astype(BF16) for a in (ckt, cknt, cvt, cvnt))
    for j in range(3):
        q2 = att[:, 1152 + BLOCK * j:1152 + BLOCK * (j + 1)]
        outs = []
        for e in range(2):
            h = 2 * j + e
            g = h // 3
            qa = q2 if e == g else pltpu.roll(q2, HEAD_DIM, 1)
            in_group = (lane >= HEAD_DIM * g) & (lane < HEAD_DIM * (g + 1))
            qm = jnp.where(in_group, qa, 0.0).astype(BF16)
            o = attend(qm, *bf, w_c, sinks_ref[h])
            outs.append(o if e == g else pltpu.roll(o, HEAD_DIM, 1))
        yc_ref[0, :, BLOCK * j:BLOCK * (j + 1)] = jnp.where(lane < HEAD_DIM, outs[0], outs[1]).astype(BF16)
    for old, new_t, o_ref in ((ckt, cknt, ockt_ref), (cvt, cvnt, ocvt_ref)):
        _, tail = shifted(old, new_t)
        o_ref[0, 0] = tail


def _sample_attn(layer, att, sinks, kt, vt, ckt, cvt, masks, prev_outs):
    n, ds, _ = att.shape
    big = pl.BlockSpec((1, 1, B_WIDTH, WB), lambda b: (layer, b, 0, 0))
    small = pl.BlockSpec((1, 1, C_KV_WIDTH, WC), lambda b: (layer, b, 0, 0))
    tok = lambda width: pl.BlockSpec((1, ds, width), lambda b: (b, 0, 0))
    const = lambda a: pl.BlockSpec(a.shape, lambda b: (0, 0))
    in_specs = [pl.BlockSpec(memory_space=pltpu.SMEM), tok(ATT_WIDTH), big, big, small, small,
                const(masks[0]), const(masks[1])]
    args = [sinks, att, kt, vt, ckt, cvt, masks[0], masks[1]]
    aliases = {}
    if prev_outs is not None:
        in_specs += [pl.BlockSpec(memory_space=pl.ANY)] * 4
        aliases = {len(args) + i: 2 + i for i in range(4)}
        args += list(prev_outs)
    return pl.pallas_call(
        functools.partial(_sample_attn_kernel, layer),
        grid=(n,),
        in_specs=in_specs,
        out_specs=[tok(B_WIDTH), tok(C_WIDTH), big, big, small, small],
        out_shape=[jax.ShapeDtypeStruct((n, ds, B_WIDTH), BF16),
                   jax.ShapeDtypeStruct((n, ds, C_WIDTH), BF16),
                   jax.ShapeDtypeStruct(kt.shape, F32), jax.ShapeDtypeStruct(vt.shape, F32),
                   jax.ShapeDtypeStruct(ckt.shape, F32), jax.ShapeDtypeStruct(cvt.shape, F32)],
        input_output_aliases=aliases,
        compiler_params=_params(1),
        name="sample_attn",
    )(*args)


def _mlp_kernel(final, x_ref, ya_ref, yb_ref, yc_ref, gate1_ref, shift2_ref, scale2_ref, gate2_ref,
                g_ref, gf_ref, wo_ref, wgu_ref, wd_ref, o_ref, acc_ref):
    nb, tt, _ = x_ref.shape
    m = nb * tt
    mix = jnp.concatenate([ya_ref[...], yb_ref[...], yc_ref[...]], axis=-1).reshape(m, D_MODEL)
    attn = jnp.dot(mix, wo_ref[...], preferred_element_type=F32).reshape(nb, tt, D_MODEL)
    x1 = x_ref[...] + gate1_ref[...] * attn
    hb = _rms_mod(x1, g_ref[...], scale2_ref[...], shift2_ref[...]).reshape(m, D_MODEL).astype(BF16)
    for c in range(D_FF // FF_CHUNK):
        lo = c * FF_CHUNK
        gate = jnp.dot(hb, wgu_ref[:, lo:lo + FF_CHUNK], preferred_element_type=F32)
        up = jnp.dot(hb, wgu_ref[:, D_FF + lo:D_FF + lo + FF_CHUNK], preferred_element_type=F32)
        act = (gate * jax.nn.sigmoid(gate) * up).astype(BF16)
        part = jnp.dot(act, wd_ref[lo:lo + FF_CHUNK, :], preferred_element_type=F32)
        if c == 0:
            acc_ref[...] = part
        else:
            acc_ref[...] += part
    x2 = x1 + gate2_ref[...] * acc_ref[...].reshape(nb, tt, D_MODEL)
    if final:
        ms = jnp.mean(x2 * x2, axis=-1, keepdims=True)
        x2 = x2 * lax.rsqrt(ms + EPS) * gf_ref[...]
    o_ref[...] = x2


def _mlp(x, ya, yb, yc, mod3, g_ffn, g_final, w_out, w_gu, w_down, *, nb, tt, final):
    n, s, _ = x.shape
    tok = lambda width: pl.BlockSpec((nb, tt, width), lambda b, t: (b, t, 0))
    modspec = lambda k: pl.BlockSpec((nb, 1, D_MODEL), lambda b, t: (b, 0, k))
    vec = pl.BlockSpec((1, D_MODEL), lambda b, t: (0, 0))
    weight = lambda a: pl.BlockSpec(a.shape, lambda b, t: (0, 0), pipeline_mode=pl.Buffered(1))
    return pl.pallas_call(
        functools.partial(_mlp_kernel, final),
        grid=(n // nb, s // tt),
        in_specs=[tok(D_MODEL), tok(A_WIDTH), tok(B_WIDTH), tok(C_WIDTH),
                  modspec(2), modspec(3), modspec(4), modspec(5), vec, vec,
                  weight(w_out), weight(w_gu), weight(w_down)],
        out_specs=tok(D_MODEL),
        out_shape=jax.ShapeDtypeStruct(x.shape, F32),
        scratch_shapes=[pltpu.VMEM((nb * tt, D_MODEL), F32)],
        compiler_params=_params(2),
        name="outproj_swiglu",
    )(x, ya, yb, yc, mod3, mod3, mod3, mod3, g_ffn, g_final, w_out, w_gu, w_down)


def _to_slots_minor(cache):
    depth, n, w, h, hd = cache.shape
    return jnp.transpose(cache, (0, 1, 3, 4, 2)).reshape(depth, n, h * hd, w)


def _from_slots_minor(buf, heads):
    depth, n, width, w = buf.shape
    return jnp.transpose(buf.reshape(depth, n, heads, width // heads, w), (0, 1, 4, 2, 3))


def kernel(x_prompt, x_sample, state_conv, cache_b_k, cache_b_v, cache_c_k, cache_c_v, c_prompt, c_sample,
           w_mod, b_mod, norm_mix, norm_ffn, w_in, conv_w, sinks, w_out, w_gate_up, w_down, norm_final):
    n_p, seq, _ = x_prompt.shape
    n_s, ds, _ = x_sample.shape
    tt_p = 512

    mod = _modulation(jnp.concatenate([c_prompt, c_sample], axis=0), w_mod, b_mod)
    kt, vt = _to_slots_minor(cache_b_k), _to_slots_minor(cache_b_v)
    ckt, cvt = _to_slots_minor(cache_c_k), _to_slots_minor(cache_c_v)
    masks = _sample_masks(ds)
    g_final = norm_final.reshape(1, D_MODEL)
    zero_state = jnp.zeros((n_p, 2, A_WIDTH), F32)

    y_p, y_s = x_prompt, x_sample
    p_states, s_conv, s_bufs = [], [], None
    for layer in range(DEPTH):
        mod_p = mod[layer, :n_p].reshape(n_p, 1, 6 * D_MODEL)
        mod_s = mod[layer, n_p:].reshape(n_s, 1, 6 * D_MODEL)
        g_mix = norm_mix[layer].reshape(1, D_MODEL)
        g_ffn = norm_ffn[layer].reshape(1, D_MODEL)
        w_in_l = w_in[layer].astype(BF16)
        weights = (w_out[layer].astype(BF16), w_gate_up[layer].astype(BF16), w_down[layer].astype(BF16))
        final = layer == DEPTH - 1

        ya, att, ust = _inproj(y_p, mod_p, g_mix, w_in_l, conv_w[layer], zero_state, nb=1, tt=tt_p)
        yb, yc = _prompt_attn(att, sinks[layer])
        y_p = _mlp(y_p, ya, yb, yc, mod_p, g_ffn, g_final, *weights, nb=1, tt=tt_p, final=final)
        tail = lambda lo, hi, w, heads: att[:, seq - w:, lo:hi].reshape(n_p, w, heads, HEAD_DIM)
        p_states.append((ust, tail(384, 768, WB, 6), tail(768, 1152, WB, 6),
                         tail(1536, 1664, WC, 2), tail(1664, 1792, WC, 2)))

        ya, att, ust = _inproj(y_s, mod_s, g_mix, w_in_l, conv_w[layer], state_conv[layer], nb=n_s, tt=ds)
        yb, yc, *s_bufs = _sample_attn(layer, att, sinks[layer], kt, vt, ckt, cvt, masks, s_bufs)
        y_s = _mlp(y_s, ya, yb, yc, mod_s, g_ffn, g_final, *weights, nb=n_s, tt=ds, final=final)
        s_conv.append(ust)

    conv_p, bk_p, bv_p, ck_p, cv_p = [jnp.stack(t) for t in zip(*p_states)]
    conv_s = jnp.stack(s_conv)
    bk_s, bv_s = _from_slots_minor(s_bufs[0], 6), _from_slots_minor(s_bufs[1], 6)
    ck_s, cv_s = _from_slots_minor(s_bufs[2], 2), _from_slots_minor(s_bufs[3], 2)
    return (y_p, y_s, conv_p, conv_s, bk_p, bk_s, bv_p, bv_s, ck_p, ck_s, cv_p, cv_s)
```

```python
import functools

import numpy as np
import jax
import jax.numpy as jnp
from jax import lax
from jax.experimental import pallas as pl
from jax.experimental.pallas import tpu as pltpu

D_MODEL = 1024
DEPTH = 2
HEAD_DIM = 64
A_WIDTH = 256
B_WIDTH = 384
C_WIDTH = 384
C_KV_WIDTH = 128
IN_WIDTH = 2560
ATT_WIDTH = 1792
D_FF = 2816
FF_CHUNK = 256
BLOCK = 128
ATTN_UNITS = 4
WB = 2048
WC = 128
EPS = 1e-6
ATTN_SCALE = HEAD_DIM ** -0.5
DILATIONS = ((128, 1), (512, 4), (2048, 16))
NEG_INF = float("-inf")
F32 = jnp.float32
BF16 = jnp.bfloat16
NT_DIMS = (((1,), (1,)), ((), ()))
VMEM_LIMIT = 56 * 1024 * 1024


def _params(n_axes, vmem=VMEM_LIMIT):
    return pltpu.CompilerParams(dimension_semantics=("arbitrary",) * n_axes, vmem_limit_bytes=vmem)


def _rms_mod(x, g, scale, shift):
    ms = jnp.mean(x * x, axis=-1, keepdims=True)
    return (x * lax.rsqrt(ms + EPS) * g) * (1.0 + scale) + shift


def _mod_kernel(c_ref, w_ref, b_ref, o_ref):
    c = c_ref[...]
    a = (c * jax.nn.sigmoid(c)).astype(BF16)
    o_ref[0] = jnp.dot(a, w_ref[0].astype(BF16), preferred_element_type=F32) + b_ref[0]


def _modulation(c_all, w_mod, b_mod):
    n = c_all.shape[0]
    return pl.pallas_call(
        _mod_kernel,
        grid=(DEPTH, 6),
        in_specs=[pl.BlockSpec((n, D_MODEL), lambda l, k: (0, 0)),
                  pl.BlockSpec((1, D_MODEL, D_MODEL), lambda l, k: (l, 0, k)),
                  pl.BlockSpec((1, 1, D_MODEL), lambda l, k: (l, 0, k))],
        out_specs=pl.BlockSpec((1, n, D_MODEL), lambda l, k: (l, 0, k)),
        out_shape=jax.ShapeDtypeStruct((DEPTH, n, 6 * D_MODEL), F32),
        compiler_params=_params(2),
        name="adaln_mod",
    )(c_all, w_mod, b_mod.reshape(DEPTH, 1, 6 * D_MODEL))


def _inproj_kernel(x_ref, shift_ref, scale_ref, g_ref, w_ref, cw_ref, st_ref,
                   ya_ref, att_ref, ust_ref, carry_ref):
    nb, tt, _ = x_ref.shape
    m = nb * tt

    @pl.when(pl.program_id(1) == 0)
    def _():
        carry_ref[...] = st_ref[...]

    h = _rms_mod(x_ref[...], g_ref[...], scale_ref[...], shift_ref[...])
    hb = h.reshape(m, D_MODEL).astype(BF16)

    def proj(lo, hi):
        return jnp.dot(hb, w_ref[:, lo:hi], preferred_element_type=F32)

    pa = proj(0, 3 * A_WIDTH)
    ga = pa[:, 0:A_WIDTH].reshape(nb, tt, A_WIDTH)
    u = (pa[:, A_WIDTH:2 * A_WIDTH] * pa[:, 2 * A_WIDTH:3 * A_WIDTH]).reshape(nb, tt, A_WIDTH)
    c0 = carry_ref[:, 0:1, :]
    c1 = carry_ref[:, 1:2, :]
    row = lax.broadcasted_iota(jnp.int32, u.shape, 1)
    u1 = jnp.where(row == 0, c1, pltpu.roll(u, 1, 1))
    u2 = jnp.where(row == 0, c0, jnp.where(row == 1, c1, pltpu.roll(u, 2, 1)))
    cw = cw_ref[...]
    conv = cw[0:1, :] * u2 + cw[1:2, :] * u1 + cw[2:3, :] * u
    ya_ref[...] = (ga * conv).astype(BF16)
    last = u[:, tt - 2:tt, :]
    carry_ref[...] = last
    ust_ref[...] = last

    att_ref[:, :, 0:384] = (proj(768, 1152) * ATTN_SCALE).reshape(nb, tt, 384)
    att_ref[:, :, 384:1152] = proj(1152, 1920).reshape(nb, tt, 768)
    att_ref[:, :, 1152:1536] = (proj(1920, 2304) * ATTN_SCALE).reshape(nb, tt, 384)
    att_ref[:, :, 1536:1792] = proj(2304, 2560).reshape(nb, tt, 256)


def _inproj(x, mod3, g, w_in, conv_w, state, *, nb, tt):
    n, s, _ = x.shape
    grid = (n // nb, s // tt)
    tok = lambda width: pl.BlockSpec((nb, tt, width), lambda b, t: (b, t, 0))
    modspec = lambda k: pl.BlockSpec((nb, 1, D_MODEL), lambda b, t: (b, 0, k))
    per_seq = pl.BlockSpec((nb, 2, A_WIDTH), lambda b, t: (b, 0, 0))
    return pl.pallas_call(
        _inproj_kernel,
        grid=grid,
        in_specs=[tok(D_MODEL), modspec(0), modspec(1),
                  pl.BlockSpec((1, D_MODEL), lambda b, t: (0, 0)),
                  pl.BlockSpec((D_MODEL, IN_WIDTH), lambda b, t: (0, 0)),
                  pl.BlockSpec((3, A_WIDTH), lambda b, t: (0, 0)),
                  per_seq],
        out_specs=[tok(A_WIDTH), tok(ATT_WIDTH), per_seq],
        out_shape=[jax.ShapeDtypeStruct((n, s, A_WIDTH), BF16),
                   jax.ShapeDtypeStruct((n, s, ATT_WIDTH), F32),
                   jax.ShapeDtypeStruct((n, 2, A_WIDTH), F32)],
        scratch_shapes=[pltpu.VMEM((nb, 2, A_WIDTH), F32)],
        compiler_params=_params(2),
        name="inproj_conv",
    )(x, mod3, mod3, g, w_in, conv_w, state)


def _staggered(first_stage, later_stages):
    n_stages = 1 + len(later_stages)
    live = {}
    for step in range(ATTN_UNITS + n_stages - 1):
        for k in reversed(range(n_stages)):
            u = step - k
            if 0 <= u < ATTN_UNITS:
                live[u] = first_stage(u) if k == 0 else later_stages[k - 1](live[u])


def _prompt_attn_kernel(sinks_ref, qb_ref, kb_ref, vb_ref, qc_ref, kc_ref, vc_ref,
                        yb_ref, yc_ref, st_ref, kw_ref, vw_ref):
    j = pl.program_id(1)
    n_blocks = qb_ref.shape[1] // BLOCK
    lane = lax.broadcasted_iota(jnp.int32, (BLOCK, BLOCK), 1)
    head0 = lane < HEAD_DIM
    qi = lax.broadcasted_iota(jnp.int32, (2 * BLOCK, BLOCK), 0) & (BLOCK - 1)
    kj = lax.broadcasted_iota(jnp.int32, (2 * BLOCK, BLOCK), 1)
    as_bias = lambda valid: jnp.where(valid, 0.0, NEG_INF)
    bias_cur = as_bias(kj <= qi)
    bias_prev_dilated = as_bias(kj >= qi)
    bias_prev_window = as_bias(kj > qi)

    def stack_heads(q):
        return jnp.concatenate([jnp.where(head0, q, 0.0), jnp.where(head0, 0.0, q)], axis=0).astype(BF16)

    def unstack(x):
        return jnp.where(head0, x[:BLOCK], x[BLOCK:])

    def wide(a):
        return jnp.concatenate([a, a], axis=1)

    def masked_scores(q2, kk, mb, bias_prev):
        s = lax.dot_general(q2, kk, NT_DIMS, preferred_element_type=F32)
        no_prev = jnp.where(mb > 0, 0.0, NEG_INF)
        return s + jnp.concatenate([bias_prev + no_prev, bias_cur], axis=1)

    def dilation_pass(d, first, last):
        per_phase = n_blocks // d
        shift = per_phase.bit_length() - 1

        def rows_of(start):
            if d == 1:
                return pl.ds(pl.multiple_of(start, BLOCK), BLOCK)
            return pl.ds(start, BLOCK, stride=d)

        def load(it, u):
            idx = it * ATTN_UNITS + u
            phase = idx >> shift
            mb = idx & (per_phase - 1)
            rows = rows_of(phase + d * BLOCK * mb)
            prev = rows_of(phase + d * BLOCK * jnp.maximum(mb - 1, 0))
            q2 = stack_heads(qb_ref[0, rows, :])
            kk = jnp.concatenate([kb_ref[0, prev, :], kb_ref[0, rows, :]], axis=0).astype(BF16)
            vv = jnp.concatenate([vb_ref[0, prev, :], vb_ref[0, rows, :]], axis=0).astype(BF16)
            return dict(rows=rows, vv=vv, s=masked_scores(q2, kk, mb, bias_prev_dilated))

        def softmax(c):
            m_cur = jnp.max(c["s"], axis=1, keepdims=True)
            if first:
                c["m_new"] = jnp.broadcast_to(m_cur, (2 * BLOCK, BLOCK))
            else:
                c["old"] = [st_ref[k, c["rows"], :] for k in range(5)]
                c["m_old"] = jnp.concatenate(c["old"][0:2], axis=0)
                c["m_new"] = jnp.maximum(c["m_old"], m_cur)
            p = jnp.exp(c.pop("s") - wide(c["m_new"]))
            c["l_cur"] = jnp.sum(p, axis=1, keepdims=True)
            c["p"] = p.astype(BF16)
            return c

        def weighted(c):
            c["pv"] = jnp.dot(c.pop("p"), c.pop("vv"), preferred_element_type=F32)
            return c

        def update(c):
            rows = c["rows"]
            if first:
                l_new = jnp.broadcast_to(c["l_cur"], (2 * BLOCK, BLOCK))
                acc = unstack(c["pv"])
            else:
                old = c["old"]
                alpha = jnp.exp(c["m_old"] - c["m_new"])
                l_new = alpha * jnp.concatenate(old[2:4], axis=0) + c["l_cur"]
                acc = unstack(alpha * jnp.concatenate([old[4]] * 2, axis=0) + c["pv"])
            if last:
                yb_ref[0, rows, :] = (acc / unstack(l_new)).astype(BF16)
            else:
                new = (c["m_new"][:BLOCK], c["m_new"][BLOCK:], l_new[:BLOCK], l_new[BLOCK:], acc)
                for k in range(5):
                    st_ref[k, rows, :] = new[k]
            return c

        def body(it, carry):
            _staggered(functools.partial(load, it), (softmax, weighted, update))
            return carry

        lax.fori_loop(0, n_blocks // ATTN_UNITS, body, 0)

    order = sorted(d for _, d in DILATIONS)[::-1]
    for i, d in enumerate(order):
        dilation_pass(d, i == 0, i == len(order) - 1)

    def arrange(x_ref, o_ref):
        def body(it, carry):
            for u in range(ATTN_UNITS):
                rows = pl.ds(pl.multiple_of((it * ATTN_UNITS + u) * BLOCK, BLOCK), BLOCK)
                x = x_ref[0, rows, :]
                xs = pltpu.roll(x, HEAD_DIM, 1)
                o_ref[rows, :] = jnp.where(head0, jnp.where(j == 2, xs, x), jnp.where(j == 0, xs, x)).astype(BF16)
            return carry
        lax.fori_loop(0, n_blocks // ATTN_UNITS, body, 0)

    arrange(kc_ref, kw_ref)
    arrange(vc_ref, vw_ref)
    row2 = lax.broadcasted_iota(jnp.int32, (2 * BLOCK, 1), 0)
    sink = jnp.where(row2 < BLOCK, sinks_ref[2 * j], sinks_ref[2 * j + 1])

    def window_load(it, u):
        mb = it * ATTN_UNITS + u
        rows = pl.ds(pl.multiple_of(mb * BLOCK, BLOCK), BLOCK)
        prev = pl.ds(pl.multiple_of(jnp.maximum(mb - 1, 0) * BLOCK, BLOCK), BLOCK)
        q2 = stack_heads(qc_ref[0, rows, :])
        kk = jnp.concatenate([kw_ref[prev, :], kw_ref[rows, :]], axis=0)
        vv = jnp.concatenate([vw_ref[prev, :], vw_ref[rows, :]], axis=0)
        return dict(rows=rows, vv=vv, s=masked_scores(q2, kk, mb, bias_prev_window))

    def window_softmax(c):
        m = jnp.maximum(jnp.max(c["s"], axis=1, keepdims=True), sink)
        p = jnp.exp(c.pop("s") - m)
        c["l"] = jnp.sum(p, axis=1, keepdims=True) + jnp.exp(sink - m)
        c["p"] = p.astype(BF16)
        return c

    def window_out(c):
        pv = jnp.dot(c.pop("p"), c.pop("vv"), preferred_element_type=F32)
        yc_ref[0, c["rows"], :] = unstack(pv / c["l"]).astype(BF16)
        return c

    def window_body(it, carry):
        _staggered(functools.partial(window_load, it), (window_softmax, window_out))
        return carry

    lax.fori_loop(0, n_blocks // ATTN_UNITS, window_body, 0)


def _prompt_attn(att, sinks):
    n, s, _ = att.shape
    lanes = lambda f: pl.BlockSpec((1, s, BLOCK), f)
    out = pl.BlockSpec((1, s, BLOCK), lambda b, j: (b, 0, j))
    return pl.pallas_call(
        _prompt_attn_kernel,
        grid=(n, 3),
        in_specs=[pl.BlockSpec(memory_space=pltpu.SMEM),
                  lanes(lambda b, j: (b, 0, j)), lanes(lambda b, j: (b, 0, 3 + j)),
                  lanes(lambda b, j: (b, 0, 6 + j)), lanes(lambda b, j: (b, 0, 9 + j)),
                  lanes(lambda b, j: (b, 0, 12)), lanes(lambda b, j: (b, 0, 13))],
        out_specs=[out, out],
        out_shape=[jax.ShapeDtypeStruct((n, s, B_WIDTH), BF16),
                   jax.ShapeDtypeStruct((n, s, C_WIDTH), BF16)],
        scratch_shapes=[pltpu.VMEM((5, s, BLOCK), F32),
                        pltpu.VMEM((s, BLOCK), BF16), pltpu.VMEM((s, BLOCK), BF16)],
        compiler_params=_params(2),
        name="prompt_attn",
    )(sinks, att, att, att, att, att, att)


def _sample_masks(ds):
    i = np.arange(ds)[:, None]
    def with_new(old, new_of_delta):
        col = np.arange(BLOCK)[None, :] - (BLOCK - ds)
        delta = i - col
        new = np.where((col >= 0) & (delta >= 0), new_of_delta(np.maximum(delta, 0)), 0.0)
        return np.concatenate([old, new], axis=1).astype(np.float32)
    def mult(delta):
        return sum(((delta <= w) & (delta % d == 0)).astype(np.float32) for w, d in DILATIONS)
    mb = with_new(mult(WB + i - np.arange(WB)[None, :]), mult)
    dist = WC + i - np.arange(WC)[None, :]
    mc = with_new((dist < WC).astype(np.float32), lambda delta: np.ones_like(delta, np.float32))
    return jnp.asarray(mb), jnp.asarray(mc)


def _sample_attn_kernel(layer, sinks_ref, att_ref, kt_ref, vt_ref, ckt_ref, cvt_ref, mb_ref, mc_ref,
                        *rest):
    yb_ref, yc_ref, okt_ref, ovt_ref, ockt_ref, ocvt_ref = rest[-6:]
    del layer
    ds = att_ref.shape[1]
    att = att_ref[0]
    lane = lax.broadcasted_iota(jnp.int32, (ds, BLOCK), 1)
    lane_sq = lax.broadcasted_iota(jnp.int32, (BLOCK, BLOCK), 1)
    is_new = lane_sq >= BLOCK - ds
    pad_rows = jnp.zeros((BLOCK - ds, BLOCK), F32)

    def new_cols(x):
        return jnp.concatenate([pad_rows, x], axis=0).T

    def shifted(old, new_t):
        w = old.shape[1]
        rolled = pltpu.roll(old, w - ds, 1)
        tail = jnp.where(is_new, new_t, rolled[:, w - BLOCK:])
        return rolled, tail

    def attend(qm, kt, knt, vt, vnt, weight, sink):
        s = jnp.concatenate([jnp.dot(qm, kt, preferred_element_type=F32),
                             jnp.dot(qm, knt, preferred_element_type=F32)], axis=1)
        s = jnp.where(weight > 0.0, s, NEG_INF)
        m = jnp.max(s, axis=1, keepdims=True)
        if sink is not None:
            m = jnp.maximum(m, sink)
        p = weight * jnp.exp(s - m)
        l = jnp.sum(p, axis=1, keepdims=True)
        if sink is not None:
            l = l + jnp.exp(sink - m)
        w = kt.shape[1]
        pb = p.astype(BF16)
        o = (lax.dot_general(pb[:, :w], vt, NT_DIMS, preferred_element_type=F32)
             + lax.dot_general(pb[:, w:], vnt, NT_DIMS, preferred_element_type=F32))
        return o / l

    w_b = mb_ref[...]
    for j in range(3):
        rows = slice(BLOCK * j, BLOCK * (j + 1))
        q2 = att[:, BLOCK * j:BLOCK * (j + 1)]
        knt = new_cols(att[:, 384 + BLOCK * j:384 + BLOCK * (j + 1)])
        vnt = new_cols(att[:, 768 + BLOCK * j:768 + BLOCK * (j + 1)])
        kt = kt_ref[0, 0, rows, :]
        vt = vt_ref[0, 0, rows, :]
        bf = tuple(a.astype(BF16) for a in (kt, knt, vt, vnt))
        outs = []
        for e in range(2):
            in_head = (lane >= HEAD_DIM * e) & (lane < HEAD_DIM * (e + 1))
            qm = jnp.where(in_head, q2, 0.0).astype(BF16)
            outs.append(attend(qm, *bf, w_b, None))
        yb_ref[0, :, rows] = jnp.where(lane < HEAD_DIM, outs[0], outs[1]).astype(BF16)
        for old, new_t, o_ref in ((kt, knt, okt_ref), (vt, vnt, ovt_ref)):
            rolled, tail = shifted(old, new_t)
            o_ref[0, 0, rows, :] = rolled
            o_ref[0, 0, rows, WB - BLOCK:WB] = tail

    w_c = mc_ref[...]
    ckt = ckt_ref[0, 0]
    cvt = cvt_ref[0, 0]
    cknt = new_cols(att[:, 1536:1664])
    cvnt = new_cols(att[:, 1664:1792])
    bf = tuple(a.astype(BF16) for a in (ckt, cknt, cvt, cvnt))
    for j in range(3):
        q2 = att[:, 1152 + BLOCK * j:1152 + BLOCK * (j + 1)]
        outs = []
        for e in range(2):
            h = 2 * j + e
            g = h // 3
            qa = q2 if e == g else pltpu.roll(q2, HEAD_DIM, 1)
            in_group = (lane >= HEAD_DIM * g) & (lane < HEAD_DIM * (g + 1))
            qm = jnp.where(in_group, qa, 0.0).astype(BF16)
            o = attend(qm, *bf, w_c, sinks_ref[h])
            outs.append(o if e == g else pltpu.roll(o, HEAD_DIM, 1))
        yc_ref[0, :, BLOCK * j:BLOCK * (j + 1)] = jnp.where(lane < HEAD_DIM, outs[0], outs[1]).astype(BF16)
    for old, new_t, o_ref in ((ckt, cknt, ockt_ref), (cvt, cvnt, ocvt_ref)):
        _, tail = shifted(old, new_t)
        o_ref[0, 0] = tail


def _sample_attn(layer, att, sinks, kt, vt, ckt, cvt, masks, prev_outs):
    n, ds, _ = att.shape
    big = pl.BlockSpec((1, 1, B_WIDTH, WB), lambda b: (layer, b, 0, 0))
    small = pl.BlockSpec((1, 1, C_KV_WIDTH, WC), lambda b: (layer, b, 0, 0))
    tok = lambda width: pl.BlockSpec((1, ds, width), lambda b: (b, 0, 0))
    const = lambda a: pl.BlockSpec(a.shape, lambda b: (0, 0))
    in_specs = [pl.BlockSpec(memory_space=pltpu.SMEM), tok(ATT_WIDTH), big, big, small, small,
                const(masks[0]), const(masks[1])]
    args = [sinks, att, kt, vt, ckt, cvt, masks[0], masks[1]]
    aliases = {}
    if prev_outs is not None:
        in_specs += [pl.BlockSpec(memory_space=pl.ANY)] * 4
        aliases = {len(args) + i: 2 + i for i in range(4)}
        args += list(prev_outs)
    return pl.pallas_call(
        functools.partial(_sample_attn_kernel, layer),
        grid=(n,),
        in_specs=in_specs,
        out_specs=[tok(B_WIDTH), tok(C_WIDTH), big, big, small, small],
        out_shape=[jax.ShapeDtypeStruct((n, ds, B_WIDTH), BF16),
                   jax.ShapeDtypeStruct((n, ds, C_WIDTH), BF16),
                   jax.ShapeDtypeStruct(kt.shape, F32), jax.ShapeDtypeStruct(vt.shape, F32),
                   jax.ShapeDtypeStruct(ckt.shape, F32), jax.ShapeDtypeStruct(cvt.shape, F32)],
        input_output_aliases=aliases,
        compiler_params=_params(1),
        name="sample_attn",
    )(*args)


def _mlp_kernel(final, x_ref, ya_ref, yb_ref, yc_ref, gate1_ref, shift2_ref, scale2_ref, gate2_ref,
                g_ref, gf_ref, wo_ref, wgu_ref, wd_ref, o_ref, acc_ref):
    nb, tt, _ = x_ref.shape
    m = nb * tt
    mix = jnp.concatenate([ya_ref[...], yb_ref[...], yc_ref[...]], axis=-1).reshape(m, D_MODEL)
    attn = jnp.dot(mix, wo_ref[...], preferred_element_type=F32).reshape(nb, tt, D_MODEL)
    x1 = x_ref[...] + gate1_ref[...] * attn
    hb = _rms_mod(x1, g_ref[...], scale2_ref[...], shift2_ref[...]).reshape(m, D_MODEL).astype(BF16)
    for c in range(D_FF // FF_CHUNK):
        lo = c * FF_CHUNK
        gate = jnp.dot(hb, wgu_ref[:, lo:lo + FF_CHUNK], preferred_element_type=F32)
        up = jnp.dot(hb, wgu_ref[:, D_FF + lo:D_FF + lo + FF_CHUNK], preferred_element_type=F32)
        act = (gate * jax.nn.sigmoid(gate) * up).astype(BF16)
        part = jnp.dot(act, wd_ref[lo:lo + FF_CHUNK, :], preferred_element_type=F32)
        if c == 0:
            acc_ref[...] = part
        else:
            acc_ref[...] += part
    x2 = x1 + gate2_ref[...] * acc_ref[...].reshape(nb, tt, D_MODEL)
    if final:
        ms = jnp.mean(x2 * x2, axis=-1, keepdims=True)
        x2 = x2 * lax.rsqrt(ms + EPS) * gf_ref[...]
    o_ref[...] = x2


def _mlp(x, ya, yb, yc, mod3, g_ffn, g_final, w_out, w_gu, w_down, *, nb, tt, final):
    n, s, _ = x.shape
    tok = lambda width: pl.BlockSpec((nb, tt, width), lambda b, t: (b, t, 0))
    modspec = lambda k: pl.BlockSpec((nb, 1, D_MODEL), lambda b, t: (b, 0, k))
    vec = pl.BlockSpec((1, D_MODEL), lambda b, t: (0, 0))
    weight = lambda a: pl.BlockSpec(a.shape, lambda b, t: (0, 0), pipeline_mode=pl.Buffered(1))
    return pl.pallas_call(
        functools.partial(_mlp_kernel, final),
        grid=(n // nb, s // tt),
        in_specs=[tok(D_MODEL), tok(A_WIDTH), tok(B_WIDTH), tok(C_WIDTH),
                  modspec(2), modspec(3), modspec(4), modspec(5), vec, vec,
                  weight(w_out), weight(w_gu), weight(w_down)],
        out_specs=tok(D_MODEL),
        out_shape=jax.ShapeDtypeStruct(x.shape, F32),
        scratch_shapes=[pltpu.VMEM((nb * tt, D_MODEL), F32)],
        compiler_params=_params(2),
        name="outproj_swiglu",
    )(x, ya, yb, yc, mod3, mod3, mod3, mod3, g_ffn, g_final, w_out, w_gu, w_down)


def _to_slots_minor(cache):
    depth, n, w, h, hd = cache.shape
    return jnp.transpose(cache, (0, 1, 3, 4, 2)).reshape(depth, n, h * hd, w)


def _from_slots_minor(buf, heads):
    depth, n, width, w = buf.shape
    return jnp.transpose(buf.reshape(depth, n, heads, width // heads, w), (0, 1, 4, 2, 3))


def kernel(x_prompt, x_sample, state_conv, cache_b_k, cache_b_v, cache_c_k, cache_c_v, c_prompt, c_sample,
           w_mod, b_mod, norm_mix, norm_ffn, w_in, conv_w, sinks, w_out, w_gate_up, w_down, norm_final):
    n_p, seq, _ = x_prompt.shape
    n_s, ds, _ = x_sample.shape
    tt_p = 512

    mod = _modulation(jnp.concatenate([c_prompt, c_sample], axis=0), w_mod, b_mod)
    kt, vt = _to_slots_minor(cache_b_k), _to_slots_minor(cache_b_v)
    ckt, cvt = _to_slots_minor(cache_c_k), _to_slots_minor(cache_c_v)
    masks = _sample_masks(ds)
    g_final = norm_final.reshape(1, D_MODEL)
    zero_state = jnp.zeros((n_p, 2, A_WIDTH), F32)

    y_p, y_s = x_prompt, x_sample
    p_states, s_conv, s_bufs = [], [], None
    for layer in range(DEPTH):
        mod_p = mod[layer, :n_p].reshape(n_p, 1, 6 * D_MODEL)
        mod_s = mod[layer, n_p:].reshape(n_s, 1, 6 * D_MODEL)
        g_mix = norm_mix[layer].reshape(1, D_MODEL)
        g_ffn = norm_ffn[layer].reshape(1, D_MODEL)
        w_in_l = w_in[layer].astype(BF16)
        weights = (w_out[layer].astype(BF16), w_gate_up[layer].astype(BF16), w_down[layer].astype(BF16))
        final = layer == DEPTH - 1

        ya, att, ust = _inproj(y_p, mod_p, g_mix, w_in_l, conv_w[layer], zero_state, nb=1, tt=tt_p)
        yb, yc = _prompt_attn(att, sinks[layer])
        y_p = _mlp(y_p, ya, yb, yc, mod_p, g_ffn, g_final, *weights, nb=1, tt=tt_p, final=final)
        tail = lambda lo, hi, w, heads: att[:, seq - w:, lo:hi].reshape(n_p, w, heads, HEAD_DIM)
        p_states.append((ust, tail(384, 768, WB, 6), tail(768, 1152, WB, 6),
                         tail(1536, 1664, WC, 2), tail(1664, 1792, WC, 2)))

        ya, att, ust = _inproj(y_s, mod_s, g_mix, w_in_l, conv_w[layer], state_conv[layer], nb=n_s, tt=ds)
        yb, yc, *s_bufs = _sample_attn(layer, att, sinks[layer], kt, vt, ckt, cvt, masks, s_bufs)
        y_s = _mlp(y_s, ya, yb, yc, mod_s, g_ffn, g_final, *weights, nb=n_s, tt=ds, final=final)
        s_conv.append(ust)

    conv_p, bk_p, bv_p, ck_p, cv_p = [jnp.stack(t) for t in zip(*p_states)]
    conv_s = jnp.stack(s_conv)
    bk_s, bv_s = _from_slots_minor(s_bufs[0], 6), _from_slots_minor(s_bufs[1], 6)
    ck_s, cv_s = _from_slots_minor(s_bufs[2], 2), _from_slots_minor(s_bufs[3], 2)
    return (y_p, y_s, conv_p, conv_s, bk_p, bk_s, bv_p, bv_s, ck_p, ck_s, cv_p, cv_s)
```

```python
import functools

import numpy as np
import jax
import jax.numpy as jnp
from jax import lax
from jax.experimental import pallas as pl
from jax.experimental.pallas import tpu as pltpu

D_MODEL = 1024
DEPTH = 2
HEAD_DIM = 64
A_WIDTH = 256
B_WIDTH = 384
C_WIDTH = 384
C_KV_WIDTH = 128
IN_WIDTH = 2560
ATT_WIDTH = 1792
D_FF = 2816
FF_CHUNK = 256
BLOCK = 128
ATTN_UNITS = 4
WB = 2048
WC = 128
EPS = 1e-6
ATTN_SCALE = HEAD_DIM ** -0.5
LOG2_E = 1.4426950408889634
DILATIONS = ((128, 1), (512, 4), (2048, 16))
NEG_INF = float("-inf")
F32 = jnp.float32
BF16 = jnp.bfloat16
NT_DIMS = (((1,), (1,)), ((), ()))
VMEM_LIMIT = 56 * 1024 * 1024


def _params(n_axes, vmem=VMEM_LIMIT):
    return pltpu.CompilerParams(dimension_semantics=("arbitrary",) * n_axes, vmem_limit_bytes=vmem)


def _rms_mod(x, g, scale, shift):
    ms = jnp.mean(x * x, axis=-1, keepdims=True)
    return (x * lax.rsqrt(ms + EPS) * g) * (1.0 + scale) + shift


def _mod_kernel(c_ref, w_ref, b_ref, o_ref):
    c = c_ref[...]
    a = (c * jax.nn.sigmoid(c)).astype(BF16)
    o_ref[0] = jnp.dot(a, w_ref[0].astype(BF16), preferred_element_type=F32) + b_ref[0]


def _modulation(c_all, w_mod, b_mod):
    n = c_all.shape[0]
    return pl.pallas_call(
        _mod_kernel,
        grid=(DEPTH, 6),
        in_specs=[pl.BlockSpec((n, D_MODEL), lambda l, k: (0, 0)),
                  pl.BlockSpec((1, D_MODEL, D_MODEL), lambda l, k: (l, 0, k)),
                  pl.BlockSpec((1, 1, D_MODEL), lambda l, k: (l, 0, k))],
        out_specs=pl.BlockSpec((1, n, D_MODEL), lambda l, k: (l, 0, k)),
        out_shape=jax.ShapeDtypeStruct((DEPTH, n, 6 * D_MODEL), F32),
        compiler_params=_params(2),
        name="adaln_mod",
    )(c_all, w_mod, b_mod.reshape(DEPTH, 1, 6 * D_MODEL))


def _inproj_kernel(q_scale, n_alias, x_ref, shift_ref, scale_ref, g_ref, w_ref, cw_ref, st_ref, *rest):
    ya_ref, att_ref, ust_ref, *win_refs, carry_ref = rest[n_alias:]
    nb, tt, _ = x_ref.shape
    m = nb * tt

    @pl.when(pl.program_id(1) == 0)
    def _():
        carry_ref[...] = st_ref[...]

    h = _rms_mod(x_ref[...], g_ref[...], scale_ref[...], shift_ref[...])
    hb = h.reshape(m, D_MODEL).astype(BF16)

    def proj(lo, hi):
        return jnp.dot(hb, w_ref[:, lo:hi], preferred_element_type=F32)

    pa = proj(0, 3 * A_WIDTH)
    ga = pa[:, 0:A_WIDTH].reshape(nb, tt, A_WIDTH)
    u = (pa[:, A_WIDTH:2 * A_WIDTH] * pa[:, 2 * A_WIDTH:3 * A_WIDTH]).reshape(nb, tt, A_WIDTH)
    c0 = carry_ref[:, 0:1, :]
    c1 = carry_ref[:, 1:2, :]
    row = lax.broadcasted_iota(jnp.int32, u.shape, 1)
    u1 = jnp.where(row == 0, c1, pltpu.roll(u, 1, 1))
    u2 = jnp.where(row == 0, c0, jnp.where(row == 1, c1, pltpu.roll(u, 2, 1)))
    cw = cw_ref[...]
    conv = cw[0:1, :] * u2 + cw[1:2, :] * u1 + cw[2:3, :] * u
    ya_ref[...] = (ga * conv).astype(BF16)
    last = u[:, tt - 2:tt, :]
    carry_ref[...] = last
    ust_ref[...] = last

    att_ref[:, :, 0:384] = (proj(768, 1152) * q_scale).reshape(nb, tt, 384)
    kv_b = proj(1152, 1920)
    att_ref[:, :, 384:1152] = kv_b.reshape(nb, tt, 768)
    att_ref[:, :, 1152:1536] = (proj(1920, 2304) * q_scale).reshape(nb, tt, 384)
    kv_c = proj(2304, 2560)
    att_ref[:, :, 1536:1792] = kv_c.reshape(nb, tt, 256)

    if win_refs:
        bk_ref, bv_ref, ck_ref, cv_ref = win_refs
        t = pl.program_id(1)
        n_t = pl.num_programs(1)

        @pl.when(t >= n_t - WB // tt)
        def _():
            bk_ref[0, 0] = kv_b[:, 0:B_WIDTH].T
            bv_ref[0, 0] = kv_b[:, B_WIDTH:2 * B_WIDTH].T

        @pl.when(t == n_t - 1)
        def _():
            ck_ref[0, 0] = kv_c[tt - WC:, 0:C_KV_WIDTH].T
            cv_ref[0, 0] = kv_c[tt - WC:, C_KV_WIDTH:2 * C_KV_WIDTH].T


def _inproj(layer, x, mod3, g, w_in, conv_w, state, *, nb, tt, q_scale, windows=None):
    n, s, _ = x.shape
    n_t = s // tt
    tok = lambda width: pl.BlockSpec((nb, tt, width), lambda b, t: (b, t, 0))
    modspec = lambda k: pl.BlockSpec((nb, 1, D_MODEL), lambda b, t: (b, 0, k))
    per_seq = pl.BlockSpec((nb, 2, A_WIDTH), lambda b, t: (b, 0, 0))
    in_specs = [tok(D_MODEL), modspec(0), modspec(1),
                pl.BlockSpec((1, D_MODEL), lambda b, t: (0, 0)),
                pl.BlockSpec((None, D_MODEL, IN_WIDTH), lambda b, t: (layer, 0, 0)),
                pl.BlockSpec((3, A_WIDTH), lambda b, t: (0, 0)),
                per_seq]
    args = [x, mod3, mod3, g, w_in, conv_w, state]
    out_specs = [tok(A_WIDTH), tok(ATT_WIDTH), per_seq]
    out_shape = [jax.ShapeDtypeStruct((n, s, A_WIDTH), BF16),
                 jax.ShapeDtypeStruct((n, s, ATT_WIDTH), F32),
                 jax.ShapeDtypeStruct((n, 2, A_WIDTH), F32)]
    aliases = {}
    if windows is not None:
        assert nb == 1 and WB % tt == 0 and tt >= WC
        first_tile = n_t - WB // tt
        out_specs += [pl.BlockSpec((1, 1, B_WIDTH, tt), lambda b, t: (layer, b, 0, jnp.maximum(t - first_tile, 0)))] * 2
        out_specs += [pl.BlockSpec((1, 1, C_KV_WIDTH, WC), lambda b, t: (layer, b, 0, 0))] * 2
        out_shape += [jax.ShapeDtypeStruct((DEPTH, n, B_WIDTH, WB), F32)] * 2
        out_shape += [jax.ShapeDtypeStruct((DEPTH, n, C_KV_WIDTH, WC), F32)] * 2
        if windows:
            in_specs += [pl.BlockSpec(memory_space=pl.ANY)] * 4
            aliases = {len(args) + i: 3 + i for i in range(4)}
            args += list(windows)
    return pl.pallas_call(
        functools.partial(_inproj_kernel, q_scale, len(aliases)),
        grid=(n // nb, n_t),
        in_specs=in_specs,
        out_specs=out_specs,
        out_shape=out_shape,
        scratch_shapes=[pltpu.VMEM((nb, 2, A_WIDTH), F32)],
        input_output_aliases=aliases,
        compiler_params=_params(2),
        name="inproj_conv",
    )(*args)


def _staggered(first_stage, later_stages):
    n_stages = 1 + len(later_stages)
    live = {}
    for step in range(ATTN_UNITS + n_stages - 1):
        for k in reversed(range(n_stages)):
            u = step - k
            if 0 <= u < ATTN_UNITS:
                live[u] = first_stage(u) if k == 0 else later_stages[k - 1](live[u])


def _prompt_attn_kernel(sinks_ref, qb_ref, kb_ref, vb_ref, qc_ref, kc_ref, vc_ref,
                        yb_ref, yc_ref, st_ref, kw_ref, vw_ref):
    j = pl.program_id(1)
    n_blocks = qb_ref.shape[1] // BLOCK
    lane = lax.broadcasted_iota(jnp.int32, (BLOCK, BLOCK), 1)
    head0 = lane < HEAD_DIM
    qi = lax.broadcasted_iota(jnp.int32, (2 * BLOCK, BLOCK), 0) & (BLOCK - 1)
    kj = lax.broadcasted_iota(jnp.int32, (2 * BLOCK, BLOCK), 1)
    as_bias = lambda valid: jnp.where(valid, 0.0, NEG_INF)
    bias_cur = as_bias(kj <= qi)
    bias_prev_dilated = as_bias(kj >= qi)
    bias_prev_window = as_bias(kj > qi)

    def stack_heads(q):
        return jnp.concatenate([jnp.where(head0, q, 0.0), jnp.where(head0, 0.0, q)], axis=0).astype(BF16)

    def unstack(x):
        return jnp.where(head0, x[:BLOCK], x[BLOCK:])

    def wide(a):
        return jnp.concatenate([a, a], axis=1)

    def masked_scores(q2, kk, mb, bias_prev):
        s = lax.dot_general(q2, kk, NT_DIMS, preferred_element_type=F32)
        no_prev = jnp.where(mb > 0, 0.0, NEG_INF)
        return s + jnp.concatenate([bias_prev + no_prev, bias_cur], axis=1)

    def dilation_pass(d, first, last):
        per_phase = n_blocks // d
        shift = per_phase.bit_length() - 1

        def rows_of(start):
            if d == 1:
                return pl.ds(pl.multiple_of(start, BLOCK), BLOCK)
            return pl.ds(start, BLOCK, stride=d)

        def load(it, u):
            idx = it * ATTN_UNITS + u
            phase = idx >> shift
            mb = idx & (per_phase - 1)
            rows = rows_of(phase + d * BLOCK * mb)
            prev = rows_of(phase + d * BLOCK * jnp.maximum(mb - 1, 0))
            q2 = stack_heads(qb_ref[0, rows, :])
            kk = jnp.concatenate([kb_ref[0, prev, :], kb_ref[0, rows, :]], axis=0).astype(BF16)
            vv = jnp.concatenate([vb_ref[0, prev, :], vb_ref[0, rows, :]], axis=0).astype(BF16)
            return dict(rows=rows, vv=vv, s=masked_scores(q2, kk, mb, bias_prev_dilated))

        def softmax(c):
            m_cur = jnp.max(c["s"], axis=1, keepdims=True)
            if first:
                c["m_new"] = jnp.broadcast_to(m_cur, (2 * BLOCK, BLOCK))
            else:
                c["old"] = [st_ref[k, c["rows"], :] for k in range(5)]
                c["m_old"] = jnp.concatenate(c["old"][0:2], axis=0)
                c["m_new"] = jnp.maximum(c["m_old"], m_cur)
            p = jnp.exp2(c.pop("s") - wide(c["m_new"]))
            c["l_cur"] = jnp.sum(p, axis=1, keepdims=True)
            c["p"] = p.astype(BF16)
            return c

        def weighted(c):
            c["pv"] = jnp.dot(c.pop("p"), c.pop("vv"), preferred_element_type=F32)
            return c

        def update(c):
            rows = c["rows"]
            if first:
                l_new = jnp.broadcast_to(c["l_cur"], (2 * BLOCK, BLOCK))
                acc = unstack(c["pv"])
            else:
                old = c["old"]
                alpha = jnp.exp2(c["m_old"] - c["m_new"])
                l_new = alpha * jnp.concatenate(old[2:4], axis=0) + c["l_cur"]
                acc = unstack(alpha * jnp.concatenate([old[4]] * 2, axis=0) + c["pv"])
            if last:
                yb_ref[0, rows, :] = (acc / unstack(l_new)).astype(BF16)
            else:
                new = (c["m_new"][:BLOCK], c["m_new"][BLOCK:], l_new[:BLOCK], l_new[BLOCK:], acc)
                for k in range(5):
                    st_ref[k, rows, :] = new[k]
            return c

        def body(it, carry):
            _staggered(functools.partial(load, it), (softmax, weighted, update))
            return carry

        lax.fori_loop(0, n_blocks // ATTN_UNITS, body, 0)

    order = sorted(d for _, d in DILATIONS)[::-1]
    for i, d in enumerate(order):
        dilation_pass(d, i == 0, i == len(order) - 1)

    def arrange(x_ref, o_ref):
        def body(it, carry):
            for u in range(ATTN_UNITS):
                rows = pl.ds(pl.multiple_of((it * ATTN_UNITS + u) * BLOCK, BLOCK), BLOCK)
                x = x_ref[0, rows, :]
                xs = pltpu.roll(x, HEAD_DIM, 1)
                o_ref[rows, :] = jnp.where(head0, jnp.where(j == 2, xs, x), jnp.where(j == 0, xs, x)).astype(BF16)
            return carry
        lax.fori_loop(0, n_blocks // ATTN_UNITS, body, 0)

    arrange(kc_ref, kw_ref)
    arrange(vc_ref, vw_ref)
    row2 = lax.broadcasted_iota(jnp.int32, (2 * BLOCK, 1), 0)
    sink = jnp.where(row2 < BLOCK, sinks_ref[2 * j], sinks_ref[2 * j + 1]) * LOG2_E

    def window_load(it, u):
        mb = it * ATTN_UNITS + u
        rows = pl.ds(pl.multiple_of(mb * BLOCK, BLOCK), BLOCK)
        prev = pl.ds(pl.multiple_of(jnp.maximum(mb - 1, 0) * BLOCK, BLOCK), BLOCK)
        q2 = stack_heads(qc_ref[0, rows, :])
        kk = jnp.concatenate([kw_ref[prev, :], kw_ref[rows, :]], axis=0)
        vv = jnp.concatenate([vw_ref[prev, :], vw_ref[rows, :]], axis=0)
        return dict(rows=rows, vv=vv, s=masked_scores(q2, kk, mb, bias_prev_window))

    def window_softmax(c):
        m = jnp.maximum(jnp.max(c["s"], axis=1, keepdims=True), sink)
        p = jnp.exp2(c.pop("s") - m)
        c["l"] = jnp.sum(p, axis=1, keepdims=True) + jnp.exp2(sink - m)
        c["p"] = p.astype(BF16)
        return c

    def window_out(c):
        pv = jnp.dot(c.pop("p"), c.pop("vv"), preferred_element_type=F32)
        yc_ref[0, c["rows"], :] = unstack(pv / c["l"]).astype(BF16)
        return c

    def window_body(it, carry):
        _staggered(functools.partial(window_load, it), (window_softmax, window_out))
        return carry

    lax.fori_loop(0, n_blocks // ATTN_UNITS, window_body, 0)


def _prompt_attn(att, sinks):
    n, s, _ = att.shape
    lanes = lambda f: pl.BlockSpec((1, s, BLOCK), f)
    out = pl.BlockSpec((1, s, BLOCK), lambda b, j: (b, 0, j))
    return pl.pallas_call(
        _prompt_attn_kernel,
        grid=(n, 3),
        in_specs=[pl.BlockSpec(memory_space=pltpu.SMEM),
                  lanes(lambda b, j: (b, 0, j)), lanes(lambda b, j: (b, 0, 3 + j)),
                  lanes(lambda b, j: (b, 0, 6 + j)), lanes(lambda b, j: (b, 0, 9 + j)),
                  lanes(lambda b, j: (b, 0, 12)), lanes(lambda b, j: (b, 0, 13))],
        out_specs=[out, out],
        out_shape=[jax.ShapeDtypeStruct((n, s, B_WIDTH), BF16),
                   jax.ShapeDtypeStruct((n, s, C_WIDTH), BF16)],
        scratch_shapes=[pltpu.VMEM((5, s, BLOCK), F32),
                        pltpu.VMEM((s, BLOCK), BF16), pltpu.VMEM((s, BLOCK), BF16)],
        compiler_params=_params(2),
        name="prompt_attn",
    )(sinks, att, att, att, att, att, att)


def _sample_masks(ds):
    i = np.arange(ds)[:, None]
    def with_new(old, new_of_delta):
        col = np.arange(BLOCK)[None, :] - (BLOCK - ds)
        delta = i - col
        new = np.where((col >= 0) & (delta >= 0), new_of_delta(np.maximum(delta, 0)), 0.0)
        return np.concatenate([old, new], axis=1).astype(np.float32)
    def mult(delta):
        return sum(((delta <= w) & (delta % d == 0)).astype(np.float32) for w, d in DILATIONS)
    mb = with_new(mult(WB + i - np.arange(WB)[None, :]), mult)
    dist = WC + i - np.arange(WC)[None, :]
    mc = with_new((dist < WC).astype(np.float32), lambda delta: np.ones_like(delta, np.float32))
    return jnp.asarray(mb), jnp.asarray(mc)


def _sample_attn_kernel(layer, sinks_ref, att_ref, kt_ref, vt_ref, ckt_ref, cvt_ref, mb_ref, mc_ref,
                        *rest):
    yb_ref, yc_ref, okt_ref, ovt_ref, ockt_ref, ocvt_ref = rest[-6:]
    del layer
    ds = att_ref.shape[1]
    att = att_ref[0]
    lane = lax.broadcasted_iota(jnp.int32, (ds, BLOCK), 1)
    lane_sq = lax.broadcasted_iota(jnp.int32, (BLOCK, BLOCK), 1)
    is_new = lane_sq >= BLOCK - ds
    pad_rows = jnp.zeros((BLOCK - ds, BLOCK), F32)

    def new_cols(x):
        return jnp.concatenate([pad_rows, x], axis=0).T

    def shifted(old, new_t):
        w = old.shape[1]
        rolled = pltpu.roll(old, w - ds, 1)
        tail = jnp.where(is_new, new_t, rolled[:, w - BLOCK:])
        return rolled, tail

    def attend(qm, kt, knt, vt, vnt, weight, sink):
        s = jnp.concatenate([jnp.dot(qm, kt, preferred_element_type=F32),
                             jnp.dot(qm, knt, preferred_element_type=F32)], axis=1)
        s = jnp.where(weight > 0.0, s, NEG_INF)
        m = jnp.max(s, axis=1, keepdims=True)
        if sink is not None:
            m = jnp.maximum(m, sink)
        p = weight * jnp.exp(s - m)
        l = jnp.sum(p, axis=1, keepdims=True)
        if sink is not None:
            l = l + jnp.exp(sink - m)
        w = kt.shape[1]
        pb = p.astype(BF16)
        o = (lax.dot_general(pb[:, :w], vt, NT_DIMS, preferred_element_type=F32)
             + lax.dot_general(pb[:, w:], vnt, NT_DIMS, preferred_element_type=F32))
        return o / l

    w_b = mb_ref[...]
    for j in range(3):
        rows = slice(BLOCK * j, BLOCK * (j + 1))
        q2 = att[:, BLOCK * j:BLOCK * (j + 1)]
        knt = new_cols(att[:, 384 + BLOCK * j:384 + BLOCK * (j + 1)])
        vnt = new_cols(att[:, 768 + BLOCK * j:768 + BLOCK * (j + 1)])
        kt = kt_ref[0, 0, rows, :]
        vt = vt_ref[0, 0, rows, :]
        bf = tuple(a.astype(BF16) for a in (kt, knt, vt, vnt))
        outs = []
        for e in range(2):
            in_head = (lane >= HEAD_DIM * e) & (lane < HEAD_DIM * (e + 1))
            qm = jnp.where(in_head, q2, 0.0).astype(BF16)
            outs.append(attend(qm, *bf, w_b, None))
        yb_ref[0, :, rows] = jnp.where(lane < HEAD_DIM, outs[0], outs[1]).astype(BF16)
        for old, new_t, o_ref in ((kt, knt, okt_ref), (vt, vnt, ovt_ref)):
            rolled, tail = shifted(old, new_t)
            o_ref[0, 0, rows, :] = rolled
            o_ref[0, 0, rows, WB - BLOCK:WB] = tail

    w_c = mc_ref[...]
    ckt = ckt_ref[0, 0]
    cvt = cvt_ref[0, 0]
    cknt = new_cols(att[:, 1536:1664])
    cvnt = new_cols(att[:, 1664:1792])
    bf = tuple(a.astype(BF16) for a in (ckt, cknt, cvt, cvnt))
    for j in range(3):
        q2 = att[:, 1152 + BLOCK * j:1152 + BLOCK * (j + 1)]
        outs = []
        for e in range(2):
            h = 2 * j + e
            g = h // 3
            qa = q2 if e == g else pltpu.roll(q2, HEAD_DIM, 1)
            in_group = (lane >= HEAD_DIM * g) & (lane < HEAD_DIM * (g + 1))
            qm = jnp.where(in_group, qa, 0.0).astype(BF16)
            o = attend(qm, *bf, w_c, sinks_ref[h])
            outs.append(o if e == g else pltpu.roll(o, HEAD_DIM, 1))
        yc_ref[0, :, BLOCK * j:BLOCK * (j + 1)] = jnp.where(lane < HEAD_DIM, outs[0], outs[1]).astype(BF16)
    for old, new_t, o_ref in ((ckt, cknt, ockt_ref), (cvt, cvnt, ocvt_ref)):
        _, tail = shifted(old, new_t)
        o_ref[0, 0] = tail


def _sample_attn(layer, att, sinks, kt, vt, ckt, cvt, masks, prev_outs):
    n, ds, _ = att.shape
    big = pl.BlockSpec((1, 1, B_WIDTH, WB), lambda b: (layer, b, 0, 0))
    small = pl.BlockSpec((1, 1, C_KV_WIDTH, WC), lambda b: (layer, b, 0, 0))
    tok = lambda width: pl.BlockSpec((1, ds, width), lambda b: (b, 0, 0))
    const = lambda a: pl.BlockSpec(a.shape, lambda b: (0, 0))
    in_specs = [pl.BlockSpec(memory_space=pltpu.SMEM), tok(ATT_WIDTH), big, big, small, small,
                const(masks[0]), const(masks[1])]
    args = [sinks, att, kt, vt, ckt, cvt, masks[0], masks[1]]
    aliases = {}
    if prev_outs is not None:
        in_specs += [pl.BlockSpec(memory_space=pl.ANY)] * 4
        aliases = {len(args) + i: 2 + i for i in range(4)}
        args += list(prev_outs)
    return pl.pallas_call(
        functools.partial(_sample_attn_kernel, layer),
        grid=(n,),
        in_specs=in_specs,
        out_specs=[tok(B_WIDTH), tok(C_WIDTH), big, big, small, small],
        out_shape=[jax.ShapeDtypeStruct((n, ds, B_WIDTH), BF16),
                   jax.ShapeDtypeStruct((n, ds, C_WIDTH), BF16),
                   jax.ShapeDtypeStruct(kt.shape, F32), jax.ShapeDtypeStruct(vt.shape, F32),
                   jax.ShapeDtypeStruct(ckt.shape, F32), jax.ShapeDtypeStruct(cvt.shape, F32)],
        input_output_aliases=aliases,
        compiler_params=_params(1),
        name="sample_attn",
    )(*args)


def _mlp_kernel(final, x_ref, ya_ref, yb_ref, yc_ref, gate1_ref, shift2_ref, scale2_ref, gate2_ref,
                g_ref, gf_ref, wo_ref, wgu_ref, wd_ref, o_ref, acc_ref):
    nb, tt, _ = x_ref.shape
    m = nb * tt
    mix = jnp.concatenate([ya_ref[...], yb_ref[...], yc_ref[...]], axis=-1).reshape(m, D_MODEL)
    attn = jnp.dot(mix, wo_ref[...], preferred_element_type=F32).reshape(nb, tt, D_MODEL)
    x1 = x_ref[...] + gate1_ref[...] * attn
    hb = _rms_mod(x1, g_ref[...], scale2_ref[...], shift2_ref[...]).reshape(m, D_MODEL).astype(BF16)
    for c in range(D_FF // FF_CHUNK):
        lo = c * FF_CHUNK
        gate = jnp.dot(hb, wgu_ref[:, lo:lo + FF_CHUNK], preferred_element_type=F32)
        up = jnp.dot(hb, wgu_ref[:, D_FF + lo:D_FF + lo + FF_CHUNK], preferred_element_type=F32)
        act = (gate * jax.nn.sigmoid(gate) * up).astype(BF16)
        part = jnp.dot(act, wd_ref[lo:lo + FF_CHUNK, :], preferred_element_type=F32)
        if c == 0:
            acc_ref[...] = part
        else:
            acc_ref[...] += part
    x2 = x1 + gate2_ref[...] * acc_ref[...].reshape(nb, tt, D_MODEL)
    if final:
        ms = jnp.mean(x2 * x2, axis=-1, keepdims=True)
        x2 = x2 * lax.rsqrt(ms + EPS) * gf_ref[...]
    o_ref[...] = x2


def _mlp(layer, x, ya, yb, yc, mod3, g_ffn, g_final, w_out, w_gu, w_down, *, nb, tt):
    final = layer == DEPTH - 1
    n, s, _ = x.shape
    tok = lambda width: pl.BlockSpec((nb, tt, width), lambda b, t: (b, t, 0))
    modspec = lambda k: pl.BlockSpec((nb, 1, D_MODEL), lambda b, t: (b, 0, k))
    vec = pl.BlockSpec((1, D_MODEL), lambda b, t: (0, 0))
    weight = lambda a: pl.BlockSpec((None,) + a.shape[1:], lambda b, t: (layer, 0, 0), pipeline_mode=pl.Buffered(1))
    return pl.pallas_call(
        functools.partial(_mlp_kernel, final),
        grid=(n // nb, s // tt),
        in_specs=[tok(D_MODEL), tok(A_WIDTH), tok(B_WIDTH), tok(C_WIDTH),
                  modspec(2), modspec(3), modspec(4), modspec(5), vec, vec,
                  weight(w_out), weight(w_gu), weight(w_down)],
        out_specs=tok(D_MODEL),
        out_shape=jax.ShapeDtypeStruct(x.shape, F32),
        scratch_shapes=[pltpu.VMEM((nb * tt, D_MODEL), F32)],
        compiler_params=_params(2),
        name="outproj_swiglu",
    )(x, ya, yb, yc, mod3, mod3, mod3, mod3, g_ffn, g_final, w_out, w_gu, w_down)


def _to_slots_minor(cache):
    depth, n, w, h, hd = cache.shape
    return jnp.transpose(cache, (0, 1, 3, 4, 2)).reshape(depth, n, h * hd, w)


def _from_slots_minor(buf, heads):
    depth, n, width, w = buf.shape
    return jnp.transpose(buf.reshape(depth, n, heads, width // heads, w), (0, 1, 4, 2, 3))


def kernel(x_prompt, x_sample, state_conv, cache_b_k, cache_b_v, cache_c_k, cache_c_v, c_prompt, c_sample,
           w_mod, b_mod, norm_mix, norm_ffn, w_in, conv_w, sinks, w_out, w_gate_up, w_down, norm_final):
    n_p, seq, _ = x_prompt.shape
    n_s, ds, _ = x_sample.shape
    tt_p = 512

    mod = _modulation(jnp.concatenate([c_prompt, c_sample], axis=0), w_mod, b_mod)
    kt, vt = _to_slots_minor(cache_b_k), _to_slots_minor(cache_b_v)
    ckt, cvt = _to_slots_minor(cache_c_k), _to_slots_minor(cache_c_v)
    masks = _sample_masks(ds)
    g_final = norm_final.reshape(1, D_MODEL)
    zero_state = jnp.zeros((n_p, 2, A_WIDTH), F32)

    w_in_bf = w_in.astype(BF16)
    weights = (w_out.astype(BF16), w_gate_up.astype(BF16), w_down.astype(BF16))
    y_p, y_s = x_prompt, x_sample
    p_conv, s_conv, p_bufs, s_bufs = [], [], (), None
    for layer in range(DEPTH):
        mod_p = mod[layer, :n_p].reshape(n_p, 1, 6 * D_MODEL)
        mod_s = mod[layer, n_p:].reshape(n_s, 1, 6 * D_MODEL)
        g_mix = norm_mix[layer].reshape(1, D_MODEL)
        g_ffn = norm_ffn[layer].reshape(1, D_MODEL)

        ya, att, ust, *p_bufs = _inproj(layer, y_p, mod_p, g_mix, w_in_bf, conv_w[layer], zero_state,
                                        nb=1, tt=tt_p, q_scale=ATTN_SCALE * LOG2_E, windows=p_bufs)
        yb, yc = _prompt_attn(att, sinks[layer])
        y_p = _mlp(layer, y_p, ya, yb, yc, mod_p, g_ffn, g_final, *weights, nb=1, tt=tt_p)
        p_conv.append(ust)

        ya, att, ust = _inproj(layer, y_s, mod_s, g_mix, w_in_bf, conv_w[layer], state_conv[layer],
                               nb=n_s, tt=ds, q_scale=ATTN_SCALE)
        yb, yc, *s_bufs = _sample_attn(layer, att, sinks[layer], kt, vt, ckt, cvt, masks, s_bufs)
        y_s = _mlp(layer, y_s, ya, yb, yc, mod_s, g_ffn, g_final, *weights, nb=n_s, tt=ds)
        s_conv.append(ust)

    heads = (6, 6, 2, 2)
    bk_p, bv_p, ck_p, cv_p = [_from_slots_minor(b, h) for b, h in zip(p_bufs, heads)]
    bk_s, bv_s, ck_s, cv_s = [_from_slots_minor(b, h) for b, h in zip(s_bufs, heads)]
    return (y_p, y_s, jnp.stack(p_conv), jnp.stack(s_conv), bk_p, bk_s, bv_p, bv_s, ck_p, ck_s, cv_p, cv_s)
```
